```python
import math
import jax
import jax.numpy as jnp
from jax import lax
import numpy as np

D_MODEL = 4096
BATCH = 2
SEQ = 8192
DEPTH = 2

N_BRANCH = 4
BRANCH_W = D_MODEL // N_BRANCH

M_HEADDIM = 64
M_HEADS = BRANCH_W // M_HEADDIM
M_GROUPS = 2
M_STATE = 128
M_CONV = 4
M_CHUNK = 128
M_XBC = BRANCH_W + 2 * M_GROUPS * M_STATE
M_COLS = BRANCH_W + M_XBC + M_HEADS

S5_CH = 16
S5_GROUPS = BRANCH_W // S5_CH
S5_STATE = 64
S5_COLS = BRANCH_W

R_HEADDIM = 64
R_HEADS = BRANCH_W // R_HEADDIM
R_W_RANK = 64
R_A_RANK = 64
R_V_RANK = 32
R_G_RANK = 160
R_GN_EPS = 64e-5
R_COLS = 3 * BRANCH_W + R_W_RANK + R_A_RANK + R_G_RANK

G_HEADS = 4
G_DK = BRANCH_W // 2
G_DV = BRANCH_W
G_RANK = 16
G_TAU = 16.0
G_CHUNK = 64
G_COLS = 2 * G_DK + G_DV + G_RANK + G_DV

IN_COLS = M_COLS + S5_COLS + R_COLS + G_COLS

N_GROUPS = 4
EXP_PER_GROUP = 8
N_EXPERTS = N_GROUPS * EXP_PER_GROUP
TOP_K = 2
D_EXPERT = D_MODEL // 8
MOE_BLOCK = 128

ALPHA = (2 * DEPTH) ** 0.25
BETA = (8 * DEPTH) ** -0.25
LN_EPS = 1e-5

kernel_name = 'hybrid_ssd_s5_rwkv7_gla_hmoe_deepnorm'


def _split(t, sizes):
    offs = np.cumsum(sizes)[:-1].tolist()
    return jnp.split(t, offs, axis=-1)


def layer_norm(x, w, b):
    xf = x.astype(jnp.float32)
    mu = jnp.mean(xf, -1, keepdims=True)
    var = jnp.mean(jnp.square(xf - mu), -1, keepdims=True)
    return ((xf - mu) * lax.rsqrt(var + LN_EPS) * w + b).astype(x.dtype)


def rms_norm(x, w, eps=1e-6):
    xf = x.astype(jnp.float32)
    return xf * lax.rsqrt(jnp.mean(xf * xf, -1, keepdims=True) + eps) * w


def token_shift(x):
    return jnp.pad(x, ((0, 0), (1, 0), (0, 0)))[:, :-1]


def causal_dwconv(x, w, b):
    K, C = w.shape
    out = lax.conv_general_dilated(x, w[:, None, :].astype(x.dtype), window_strides=(1,),
                                   padding=[(K - 1, 0)], dimension_numbers=('NWC', 'WIO', 'NWC'),
                                   feature_group_count=C)
    return out + b


def carry_chunk_states(decay, contrib):
    dec = jnp.moveaxis(decay, 1, 0)
    con = jnp.moveaxis(contrib, 1, 0)

    def step(s, dc):
        d, c = dc
        return d * s + c, s

    _, states = lax.scan(step, jnp.zeros_like(con[0]), (dec, con))
    return jnp.moveaxis(states, 0, 1)


def mamba2_branch(seg, conv_w, conv_b, dt_bias, a_log, d_skip, norm_w):
    B_, L, _ = seg.shape
    f32 = jnp.float32
    hg = M_HEADS // M_GROUPS
    nc = L // M_CHUNK
    z, xbc, dt = _split(seg, [BRANCH_W, M_XBC, M_HEADS])
    xbc = jax.nn.silu(causal_dwconv(xbc, conv_w, conv_b))
    xs, bm, cm = _split(xbc, [BRANCH_W, M_GROUPS * M_STATE, M_GROUPS * M_STATE])
    xs = xs.astype(f32).reshape(B_, nc, M_CHUNK, M_GROUPS, hg, M_HEADDIM)
    bm = bm.astype(f32).reshape(B_, nc, M_CHUNK, M_GROUPS, M_STATE)
    cm = cm.astype(f32).reshape(B_, nc, M_CHUNK, M_GROUPS, M_STATE)
    dt = jax.nn.softplus(dt.astype(f32) + dt_bias).reshape(B_, nc, M_CHUNK, M_GROUPS, hg)
    A = -jnp.exp(a_log.astype(f32)).reshape(M_GROUPS, hg)
    a_cs = jnp.cumsum(jnp.moveaxis(dt * A, 2, -1), -1)
    causal = jnp.tril(jnp.ones((M_CHUNK, M_CHUNK), bool))
    seg_decay = jnp.exp(jnp.where(causal, a_cs[..., :, None] - a_cs[..., None, :], -jnp.inf))
    xdt = xs * dt[..., None]
    cb = jnp.einsum('bclgn,bcsgn->bcgls', cm, bm)
    y_diag = jnp.einsum('bcgls,bcghls,bcsghp->bclghp', cb, seg_decay, xdt)
    decay_to_end = jnp.exp(a_cs[..., -1:] - a_cs)
    chunk_states = jnp.einsum('bclgn,bcghl,bclghp->bcghpn', bm, decay_to_end, xdt)
    states = carry_chunk_states(jnp.exp(a_cs[..., -1])[..., None, None], chunk_states)
    y_off = jnp.einsum('bclgn,bcghpn,bcghl->bclghp', cm, states, jnp.exp(a_cs))
    y = y_diag + y_off + xs * d_skip.astype(f32).reshape(M_GROUPS, hg)[:, :, None]
    y = y.reshape(B_, L, BRANCH_W) * jax.nn.silu(z.astype(f32))
    y = rms_norm(y.reshape(B_, L, M_GROUPS, BRANCH_W // M_GROUPS),
                 norm_w.reshape(M_GROUPS, BRANCH_W // M_GROUPS))
    return y.reshape(B_, L, BRANCH_W).astype(seg.dtype)


def s5_branch(u, lam_re, lam_im, b_re, b_im, c_re, c_im, d_skip, log_dt, glu_w, glu_b):
    B_, L, _ = u.shape
    f32 = jnp.float32
    uf = u.astype(f32)
    lr, li = lam_re.astype(f32), lam_im.astype(f32)
    dt = jnp.exp(log_dt.astype(f32))[:, None]
    mag = jnp.exp(lr * dt)
    ab_re, ab_im = mag * jnp.cos(li * dt), mag * jnp.sin(li * dt)
    den = lr * lr + li * li
    q_re = ((ab_re - 1.0) * lr + ab_im * li) / den
    q_im = (ab_im * lr - (ab_re - 1.0) * li) / den
    br, bi = b_re.astype(f32), b_im.astype(f32)
    bb_re = q_re[..., None] * br - q_im[..., None] * bi
    bb_im = q_re[..., None] * bi + q_im[..., None] * br
    ug = uf.reshape(B_, L, S5_GROUPS, S5_CH)
    bu_re = jnp.einsum('blgc,gpc->blgp', ug, bb_re)
    bu_im = jnp.einsum('blgc,gpc->blgp', ug, bb_im)
    a_re = jnp.broadcast_to(ab_re, (1, L) + ab_re.shape)
    a_im = jnp.broadcast_to(ab_im, (1, L) + ab_im.shape)

    def combine(e1, e2):
        a1r, a1i, b1r, b1i = e1
        a2r, a2i, b2r, b2i = e2
        return (a1r * a2r - a1i * a2i, a1r * a2i + a1i * a2r,
                a2r * b1r - a2i * b1i + b2r, a2r * b1i + a2i * b1r + b2i)

    _, _, h_re, h_im = lax.associative_scan(combine, (a_re, a_im, bu_re, bu_im), axis=1)
    y = (jnp.einsum('blgp,gcp->blgc', h_re, c_re.astype(f32))
         - jnp.einsum('blgp,gcp->blgc', h_im, c_im.astype(f32)))
    y = jax.nn.gelu(y.reshape(B_, L, BRANCH_W) + d_skip * uf)
    y = y * jax.nn.sigmoid(y @ glu_w.astype(f32) + glu_b)
    return y.astype(u.dtype)


def rwkv7_scan(r, w, k, v, a, b):
    B_, L, H, N = r.shape

    def step(S, inp):
        rt, wt, kt, vt, at, bt = inp
        sa = jnp.einsum('bhvk,bhk->bhv', S, at)
        S = S * wt[:, :, None, :] + sa[..., None] * bt[:, :, None, :] + vt[..., None] * kt[:, :, None, :]
        return S, jnp.einsum('bhvk,bhk->bhv', S, rt)

    xs = tuple(jnp.moveaxis(t, 1, 0) for t in (r, w, k, v, a, b))
    _, y = lax.scan(step, jnp.zeros((B_, H, N, N), jnp.float32), xs)
    return jnp.moveaxis(y, 0, 1)


def rwkv7_branch(seg, v_first, mu, w0, w2, a0, a2, g2, k_k, k_a, r_k, ln_w, ln_b, v_mix):
    B_, L, _ = seg.shape
    f32 = jnp.float32
    seg = seg + (token_shift(seg) - seg) * mu
    r, k, v, w_lo, a_lo, g_lo = _split(seg, [BRANCH_W] * 3 + [R_W_RANK, R_A_RANK, R_G_RANK])

    def heads(t):
        return t.reshape(B_, L, R_HEADS, R_HEADDIM)

    log_w = -jax.nn.softplus(-(w0 + jnp.tanh(w_lo) @ w2).astype(f32)) - 0.5
    decay = jnp.exp(-jnp.exp(log_w))
    kk = heads((k * k_k).astype(f32))
    kk = kk / jnp.maximum(jnp.sqrt(jnp.sum(kk * kk, -1, keepdims=True)), 1e-12)
    a = jax.nn.sigmoid((a0 + a_lo @ a2).astype(f32))
    r = r.astype(f32)
    k = k.astype(f32) * (1.0 + (a - 1.0) * k_a)
    v = v.astype(f32)
    if v_mix is None:
        v_first = v
    else:
        v0, v1, v2 = v_mix
        v = v + (v_first - v) * jax.nn.sigmoid((v0 + (v @ v1) @ v2).astype(f32))
    y = rwkv7_scan(heads(r), heads(decay), heads(k), heads(v), -kk, kk * heads(a))
    mu_h = jnp.mean(y, -1, keepdims=True)
    var_h = jnp.mean(jnp.square(y - mu_h), -1, keepdims=True)
    y = ((y - mu_h) * lax.rsqrt(var_h + R_GN_EPS)).reshape(B_, L, BRANCH_W) * ln_w + ln_b
    bonus = jnp.sum(heads(r * k * r_k), -1, keepdims=True) * heads(v)
    y = (y + bonus.reshape(B_, L, BRANCH_W)) * (jax.nn.sigmoid(g_lo) @ g2)
    return y.astype(seg.dtype), v_first


def gla_branch(seg, a2, a_b, norm_w):
    B_, L, _ = seg.shape
    f32 = jnp.float32
    dk, dv = G_DK // G_HEADS, G_DV // G_HEADS
    nc = L // G_CHUNK
    q, k, v, a_lo, r = _split(seg, [G_DK, G_DK, G_DV, G_RANK, G_DV])
    log_a = jax.nn.log_sigmoid((a_lo @ a2 + a_b).astype(f32)) / G_TAU

    def chunks(t, d):
        return t.astype(f32).reshape(B_, nc, G_CHUNK, G_HEADS, d).transpose(0, 1, 3, 2, 4)

    q = chunks(q, dk) * dk ** -0.5
    k = chunks(k, dk)
    v = chunks(v, dv)
    g_cs = jnp.cumsum(chunks(log_a, dk), axis=3)
    q_dec = q * jnp.exp(g_cs)
    causal = jnp.tril(jnp.ones((G_CHUNK, G_CHUNK), bool))
    att = jnp.where(causal, jnp.einsum('bnhid,bnhjd->bnhij', q_dec, k * jnp.exp(-g_cs)), 0.0)
    o = jnp.einsum('bnhij,bnhjv->bnhiv', att, v)
    contrib = jnp.einsum('bnhjd,bnhjv->bnhdv', k * jnp.exp(g_cs[..., -1:, :] - g_cs), v)
    states = carry_chunk_states(jnp.exp(g_cs[..., -1, :])[..., None], contrib)
    o = o + jnp.einsum('bnhid,bnhdv->bnhiv', q_dec, states)
    o = o.transpose(0, 1, 3, 2, 4).reshape(B_, L, G_HEADS, dv)
    o = rms_norm(o, norm_w).reshape(B_, L, G_DV) * jax.nn.silu(r.astype(f32))
    return o.astype(seg.dtype)


def moe_ffn(xt, wr_g, br_g, wr_e, br_e, w1, w3, w2):
    T, D = xt.shape
    f32 = jnp.float32
    pg = jax.nn.softmax((xt @ wr_g).astype(f32) + br_g, axis=-1)
    pg_top, grp = lax.top_k(pg, 1)
    le = ((xt @ wr_e).astype(f32) + br_e).reshape(T, N_GROUPS, EXP_PER_GROUP)
    le = le[jnp.arange(T), grp[:, 0]]
    pe_top, idx = lax.top_k(jax.nn.softmax(le, -1), TOP_K)
    gate = pg_top * pe_top / jnp.sum(pe_top, -1, keepdims=True)
    expert = (grp * EXP_PER_GROUP + idx).reshape(-1)
    n_assign = T * TOP_K
    n_blocks = -(-n_assign // MOE_BLOCK) + N_EXPERTS
    order = jnp.argsort(expert)
    e_sorted = expert[order]
    tok_sorted = order // TOP_K
    g_sorted = gate.reshape(-1)[order]
    counts = jnp.bincount(expert, length=N_EXPERTS)
    padded = (counts + MOE_BLOCK - 1) // MOE_BLOCK * MOE_BLOCK
    pad_end = jnp.cumsum(padded)
    dest = (pad_end - padded)[e_sorted] + jnp.arange(n_assign) - (jnp.cumsum(counts) - counts)[e_sorted]
    xbuf = jnp.zeros((n_blocks * MOE_BLOCK, D), xt.dtype).at[dest].set(xt[tok_sorted])
    block_expert = jnp.minimum(
        jnp.searchsorted(pad_end, jnp.arange(n_blocks) * MOE_BLOCK, side='right'), N_EXPERTS - 1)

    def expert_block(args):
        xb, e = args
        return (jax.nn.silu(xb @ w1[e]) * (xb @ w3[e])) @ w2[e]

    ybuf = lax.map(expert_block, (xbuf.reshape(n_blocks, MOE_BLOCK, D), block_expert))
    ybuf = ybuf.reshape(n_blocks * MOE_BLOCK, D)
    y = jax.ops.segment_sum(ybuf[dest] * g_sorted[:, None], tok_sorted, num_segments=T)
    return y.astype(xt.dtype)


def setup_inputs(seed: int = 0) -> dict:
    key = jax.random.key(seed)
    ks = iter(jax.random.split(key, 64))
    f32 = jnp.float32
    NL = DEPTH

    def nrm(shape, scale=1.0):
        return jax.random.normal(next(ks), shape, f32) * scale

    def unif(shape, lo, hi):
        return jax.random.uniform(next(ks), shape, f32, lo, hi)

    x = nrm((BATCH, SEQ, D_MODEL))
    w_in = nrm((NL, D_MODEL, IN_COLS), D_MODEL ** -0.5)
    m_conv_w = nrm((NL, M_CONV, M_XBC), M_CONV ** -0.5)
    m_conv_b = nrm((NL, M_XBC), 0.1)
    dt0 = jnp.exp(unif((NL, M_HEADS), math.log(1e-3), math.log(1e-1)))
    m_dt_bias = dt0 + jnp.log(-jnp.expm1(-dt0))
    m_a_log = jnp.log(unif((NL, M_HEADS), 1.0, 16.0))
    m_d = 1.0 + nrm((NL, M_HEADS), 0.1)
    m_norm_w = 1.0 + nrm((NL, BRANCH_W), 0.1)
    s5_lam_re = -0.5 + nrm((NL, S5_GROUPS, S5_STATE), 0.01)
    s5_lam_im = math.pi * jnp.arange(S5_STATE, dtype=f32) + nrm((NL, S5_GROUPS, S5_STATE), 0.01)
    s5_b_re = nrm((NL, S5_GROUPS, S5_STATE, S5_CH), (2 * S5_CH) ** -0.5)
    s5_b_im = nrm((NL, S5_GROUPS, S5_STATE, S5_CH), (2 * S5_CH) ** -0.5)
    s5_c_re = nrm((NL, S5_GROUPS, S5_CH, S5_STATE), S5_STATE ** -0.5)
    s5_c_im = nrm((NL, S5_GROUPS, S5_CH, S5_STATE), S5_STATE ** -0.5)
    s5_d = nrm((NL, BRANCH_W))
    s5_log_dt = unif((NL, S5_GROUPS), math.log(1e-3), math.log(1e-1))
    s5_glu_w = nrm((NL, BRANCH_W, BRANCH_W), BRANCH_W ** -0.5)
    s5_glu_b = nrm((NL, BRANCH_W), 0.02)
    r_mu = unif((NL, R_COLS), 0.2, 0.8)
    r_w0 = jnp.linspace(-6.0, -1.0, BRANCH_W, dtype=f32) + nrm((NL, BRANCH_W), 0.1)
    r_w2 = nrm((NL, R_W_RANK, BRANCH_W), 0.5 * R_W_RANK ** -0.5)
    r_a0 = nrm((NL, BRANCH_W), 0.1)
    r_a2 = nrm((NL, R_A_RANK, BRANCH_W), R_A_RANK ** -0.5)
    r_g2 = nrm((NL, R_G_RANK, BRANCH_W), R_G_RANK ** -0.5)
    r_k_k = 0.85 + nrm((NL, BRANCH_W), 0.05)
    r_k_a = 1.0 + nrm((NL, BRANCH_W), 0.05)
    r_r_k = nrm((NL, BRANCH_W), 0.1)
    r_ln_w = 1.0 + nrm((NL, BRANCH_W), 0.1)
    r_ln_b = nrm((NL, BRANCH_W), 0.02)
    r_v0 = 1.0 + nrm((NL - 1, BRANCH_W), 0.1)
    r_v1 = nrm((NL - 1, BRANCH_W, R_V_RANK), BRANCH_W ** -0.5)
    r_v2 = nrm((NL - 1, R_V_RANK, BRANCH_W), R_V_RANK ** -0.5)
    g_a2 = nrm((NL, G_RANK, G_DK), G_RANK ** -0.5)
    g_a_b = nrm((NL, G_DK), 0.1)
    g_norm_w = 1.0 + nrm((NL, G_DV // G_HEADS), 0.1)
    w_branch = nrm((NL, N_BRANCH, BRANCH_W, D_MODEL), BRANCH_W ** -0.5)
    w_gate = nrm((NL, N_BRANCH, D_MODEL, D_MODEL), D_MODEL ** -0.5)
    b_gate = nrm((NL, N_BRANCH, D_MODEL), 0.1)
    w_out = nrm((NL, D_MODEL, D_MODEL), BETA * D_MODEL ** -0.5)
    ln1_w = 1.0 + nrm((NL, D_MODEL), 0.1)
    ln1_b = nrm((NL, D_MODEL), 0.02)
    moe_rg_w = nrm((NL, D_MODEL, N_GROUPS), D_MODEL ** -0.5)
    moe_rg_b = nrm((NL, N_GROUPS), 0.01)
    moe_re_w = nrm((NL, D_MODEL, N_EXPERTS), D_MODEL ** -0.5)
    moe_re_b = nrm((NL, N_EXPERTS), 0.01)
    moe_w1 = nrm((NL, N_EXPERTS, D_MODEL, D_EXPERT), D_MODEL ** -0.5)
    moe_w3 = nrm((NL, N_EXPERTS, D_MODEL, D_EXPERT), D_MODEL ** -0.5)
    moe_w2 = nrm((NL, N_EXPERTS, D_EXPERT, D_MODEL), BETA * D_EXPERT ** -0.5)
    ln2_w = 1.0 + nrm((NL, D_MODEL), 0.1)
    ln2_b = nrm((NL, D_MODEL), 0.02)
    return {'x': x, 'w_in': w_in,
            'm_conv_w': m_conv_w, 'm_conv_b': m_conv_b, 'm_dt_bias': m_dt_bias, 'm_a_log': m_a_log,
            'm_d': m_d, 'm_norm_w': m_norm_w,
            's5_lam_re': s5_lam_re, 's5_lam_im': s5_lam_im, 's5_b_re': s5_b_re, 's5_b_im': s5_b_im,
            's5_c_re': s5_c_re, 's5_c_im': s5_c_im, 's5_d': s5_d, 's5_log_dt': s5_log_dt,
            's5_glu_w': s5_glu_w, 's5_glu_b': s5_glu_b,
            'r_mu': r_mu, 'r_w0': r_w0, 'r_w2': r_w2, 'r_a0': r_a0, 'r_a2': r_a2, 'r_g2': r_g2,
            'r_k_k': r_k_k, 'r_k_a': r_k_a, 'r_r_k': r_r_k, 'r_ln_w': r_ln_w, 'r_ln_b': r_ln_b,
            'r_v0': r_v0, 'r_v1': r_v1, 'r_v2': r_v2,
            'g_a2': g_a2, 'g_a_b': g_a_b, 'g_norm_w': g_norm_w,
            'w_branch': w_branch, 'w_gate': w_gate, 'b_gate': b_gate, 'w_out': w_out,
            'ln1_w': ln1_w, 'ln1_b': ln1_b,
            'moe_rg_w': moe_rg_w, 'moe_rg_b': moe_rg_b, 'moe_re_w': moe_re_w, 'moe_re_b': moe_re_b,
            'moe_w1': moe_w1, 'moe_w3': moe_w3, 'moe_w2': moe_w2, 'ln2_w': ln2_w, 'ln2_b': ln2_b}


def reference(x, w_in,
              m_conv_w, m_conv_b, m_dt_bias, m_a_log, m_d, m_norm_w,
              s5_lam_re, s5_lam_im, s5_b_re, s5_b_im, s5_c_re, s5_c_im, s5_d, s5_log_dt,
              s5_glu_w, s5_glu_b,
              r_mu, r_w0, r_w2, r_a0, r_a2, r_g2, r_k_k, r_k_a, r_r_k, r_ln_w, r_ln_b,
              r_v0, r_v1, r_v2,
              g_a2, g_a_b, g_norm_w,
              w_branch, w_gate, b_gate, w_out, ln1_w, ln1_b,
              moe_rg_w, moe_rg_b, moe_re_w, moe_re_b, moe_w1, moe_w3, moe_w2, ln2_w, ln2_b):
    v_first = None
    for l in range(DEPTH):
        proj = x @ w_in[l]
        seg_m, seg_s, seg_r, seg_g = _split(proj, [M_COLS, S5_COLS, R_COLS, G_COLS])
        y_m = mamba2_branch(seg_m, m_conv_w[l], m_conv_b[l], m_dt_bias[l], m_a_log[l], m_d[l], m_norm_w[l])
        y_s = s5_branch(seg_s, s5_lam_re[l], s5_lam_im[l], s5_b_re[l], s5_b_im[l], s5_c_re[l], s5_c_im[l],
                        s5_d[l], s5_log_dt[l], s5_glu_w[l], s5_glu_b[l])
        v_mix = None if l == 0 else (r_v0[l - 1], r_v1[l - 1], r_v2[l - 1])
        y_r, v_first = rwkv7_branch(seg_r, v_first, r_mu[l], r_w0[l], r_w2[l], r_a0[l], r_a2[l], r_g2[l],
                                    r_k_k[l], r_k_a[l], r_r_k[l], r_ln_w[l], r_ln_b[l], v_mix)
        y_g = gla_branch(seg_g, g_a2[l], g_a_b[l], g_norm_w[l])
        merged = jax.nn.sigmoid(x @ w_gate[l, 0] + b_gate[l, 0]) * (y_m @ w_branch[l, 0])
        merged = merged + jax.nn.sigmoid(x @ w_gate[l, 1] + b_gate[l, 1]) * (y_s @ w_branch[l, 1])
        merged = merged + jax.nn.sigmoid(x @ w_gate[l, 2] + b_gate[l, 2]) * (y_r @ w_branch[l, 2])
        merged = merged + jax.nn.sigmoid(x @ w_gate[l, 3] + b_gate[l, 3]) * (y_g @ w_branch[l, 3])
        x = layer_norm(ALPHA * x + merged @ w_out[l], ln1_w[l], ln1_b[l])
        ffn = moe_ffn(x.reshape(-1, D_MODEL), moe_rg_w[l], moe_rg_b[l], moe_re_w[l], moe_re_b[l],
                      moe_w1[l], moe_w3[l], moe_w2[l]).reshape(x.shape)
        x = layer_norm(ALPHA * x + ffn, ln2_w[l], ln2_b[l])
    return x
```

```python
import functools
import math

import jax
import jax.numpy as jnp
import numpy as np
from jax import lax
from jax.experimental import pallas as pl
from jax.experimental.pallas import tpu as pltpu

F32 = jnp.float32
BF16 = jnp.bfloat16
HI = lax.Precision.HIGHEST

D_MODEL = 4096
DEPTH = 2
BW = 1024
ALPHA = (2 * DEPTH) ** 0.25
LN_EPS = 1e-5

M_HEADS, M_HEADDIM, M_GROUPS, M_STATE, M_CONV, M_CHUNK = 16, 64, 2, 128, 4, 128
M_XBC = BW + 2 * M_GROUPS * M_STATE
S5_CH, S5_GROUPS, S5_STATE = 16, 64, 64
S5_GB = 8
S5_TB = 128
R_HEADS, R_HEADDIM = 16, 64
R_W_RANK, R_A_RANK, R_V_RANK, R_G_RANK = 64, 64, 32, 160
R_GN_EPS = 64e-5
R_CHUNK = 64
G_HEADS, G_DK, G_DV, G_RANK, G_TAU, G_CHUNK = 4, 512, 1024, 16, 16.0, 64
N_GROUPS, EXP_PER_GROUP, TOP_K = 4, 8, 2
N_EXPERTS = N_GROUPS * EXP_PER_GROUP
D_EXPERT = D_MODEL // 8
MOE_BM = 256

C_XBC, C_GQ, C_Z, C_S5, C_RR, C_RK, C_RV, C_GV, C_GR, C_GK, C_SM = (
    0, 1536, 2048, 3072, 4096, 5120, 6144, 7168, 8192, 9216, 9728)
P_COLS = 10240
SM_DT, SM_GA, SM_WLO, SM_ALO, SM_GLO = 0, 16, 128, 192, 256

VMEM_LIMIT_BYTES_V7X = 56 * 2 ** 20


def _params(sem, vmem=VMEM_LIMIT_BYTES_V7X):
    return pltpu.CompilerParams(dimension_semantics=sem, vmem_limit_bytes=vmem)


def _dot(a, b, precision=None):
    return jnp.dot(a, b, preferred_element_type=F32, precision=precision)


def _dot_nt(a, b, precision=None):
    return lax.dot_general(a, b, (((1,), (1,)), ((), ())), preferred_element_type=F32,
                           precision=precision)


def _dot_tn(a, b, precision=None):
    return lax.dot_general(a, b, (((0,), (0,)), ((), ())), preferred_element_type=F32,
                           precision=precision)


def _sigmoid(x):
    return 1.0 / (1.0 + jnp.exp(-x))


def _silu(x):
    return x * _sigmoid(x)


def _softplus(x):
    return jnp.maximum(x, 0.0) + jnp.log1p(jnp.exp(-jnp.abs(x)))


def _iota2(shape, dim):
    return lax.broadcasted_iota(jnp.int32, shape, dim)


def _head_expand(rows, cols, width):
    return (_iota2((rows, cols), 1) // width == _iota2((rows, cols), 0)).astype(F32)


def _head_reduce(rows, cols, width):
    return (_iota2((rows, cols), 0) // width == _iota2((rows, cols), 1)).astype(F32)


def _mm_kernel(x_ref, w_ref, o_ref):
    o_ref[...] = _dot(x_ref[...], w_ref[...]).astype(o_ref.dtype)


def _matmul(x, w, out_dtype, tm=1024, tn=512):
    m, k = x.shape
    n = w.shape[1]
    tm, tn = min(tm, m), min(tn, n)
    return pl.pallas_call(
        _mm_kernel, grid=(m // tm, n // tn),
        in_specs=[pl.BlockSpec((tm, k), lambda i, j: (i, 0)),
                  pl.BlockSpec((k, tn), lambda i, j: (0, j))],
        out_specs=pl.BlockSpec((tm, tn), lambda i, j: (i, j)),
        out_shape=jax.ShapeDtypeStruct((m, n), out_dtype),
        compiler_params=_params(("parallel", "arbitrary")), name="mm_in")(x, w)


def _gate_merge_kernel(x_ref, ym_ref, ys_ref, yr_ref, yg_ref, wg_ref, wb_ref, bg_ref, o_ref):
    x = x_ref[...]
    acc = None
    for i, y_ref in enumerate((ym_ref, ys_ref, yr_ref, yg_ref)):
        g = _dot(x, wg_ref[i]) + bg_ref[i]
        t = _sigmoid(g) * _dot(y_ref[...], wb_ref[i])
        acc = t if acc is None else acc + t
    o_ref[...] = acc.astype(o_ref.dtype)


def _gate_merge(xb, ys, wg, wb, bg, tm=512, tn=256):
    m, d = xb.shape
    tm = min(tm, m)
    row = lambda i, j: (i, 0)
    return pl.pallas_call(
        _gate_merge_kernel, grid=(m // tm, d // tn),
        in_specs=[pl.BlockSpec((tm, d), row)] + [pl.BlockSpec((tm, BW), row)] * 4 + [
            pl.BlockSpec((4, d, tn), lambda i, j: (0, 0, j)),
            pl.BlockSpec((4, BW, tn), lambda i, j: (0, 0, j)),
            pl.BlockSpec((4, 1, tn), lambda i, j: (0, 0, j))],
        out_specs=pl.BlockSpec((tm, tn), lambda i, j: (i, j)),
        out_shape=jax.ShapeDtypeStruct((m, d), BF16),
        compiler_params=_params(("parallel", "arbitrary")), name="gate_merge")(xb, *ys, wg, wb, bg)


def _mm_res_kernel(m_ref, w_ref, x_ref, o_ref):
    o_ref[...] = ALPHA * x_ref[...] + _dot(m_ref[...], w_ref[...])


def _out_proj_residual(merged, w, x, tm=1024, tn=512):
    m, k = merged.shape
    n = w.shape[1]
    tm = min(tm, m)
    return pl.pallas_call(
        _mm_res_kernel, grid=(m // tm, n // tn),
        in_specs=[pl.BlockSpec((tm, k), lambda i, j: (i, 0)),
                  pl.BlockSpec((k, tn), lambda i, j: (0, j)),
                  pl.BlockSpec((tm, tn), lambda i, j: (i, j))],
        out_specs=pl.BlockSpec((tm, tn), lambda i, j: (i, j)),
        out_shape=jax.ShapeDtypeStruct((m, n), F32),
        compiler_params=_params(("parallel", "arbitrary")), name="out_proj")(merged, w, x)


def _layer_norm(h, w, b):
    mu = jnp.mean(h, axis=-1, keepdims=True)
    d = h - mu
    var = jnp.mean(d * d, axis=-1, keepdims=True)
    return d * lax.rsqrt(var + LN_EPS) * w + b


def _route(logits):
    lane = _iota2(logits.shape, 1)
    ninf = -jnp.inf
    big = 1 << 20
    lg = jnp.where(lane < N_GROUPS, logits, ninf)
    gmax = jnp.max(lg, axis=-1, keepdims=True)
    grp = jnp.min(jnp.where(lg == gmax, lane, big), axis=-1, keepdims=True)
    sg = jnp.sum(jnp.where(lane < N_GROUPS, jnp.exp(logits - gmax), 0.0), axis=-1, keepdims=True)
    pg_top = 1.0 / sg
    lo = N_GROUPS + grp * EXP_PER_GROUP
    le = jnp.where((lane >= lo) & (lane < lo + EXP_PER_GROUP), logits, ninf)
    e1 = jnp.max(le, axis=-1, keepdims=True)
    i1 = jnp.min(jnp.where(le == e1, lane, big), axis=-1, keepdims=True)
    le2 = jnp.where(lane == i1, ninf, le)
    e2 = jnp.max(le2, axis=-1, keepdims=True)
    i2 = jnp.min(jnp.where(le2 == e2, lane, big), axis=-1, keepdims=True)
    t = jnp.exp(e2 - e1)
    g1 = pg_top / (1.0 + t)
    g2 = pg_top * t / (1.0 + t)
    out = jnp.where(lane == 0, (i1 - N_GROUPS).astype(F32), 0.0)
    out = jnp.where(lane == 1, (i2 - N_GROUPS).astype(F32), out)
    out = jnp.where(lane == 2, g1, out)
    return jnp.where(lane == 3, g2, out)


def _ln_route_kernel(h_ref, w_ref, b_ref, rw_ref, rb_ref, o32_ref, o16_ref, rt_ref):
    y = _layer_norm(h_ref[...], w_ref[...], b_ref[...])
    o32_ref[...] = y
    o16_ref[...] = y.astype(BF16)
    rt_ref[...] = _route(_dot(y, rw_ref[...], HI) + rb_ref[...])


def _ln_route(h, w, b, rw, rb, tm=256):
    m, d = h.shape
    tm = min(tm, m)
    row = lambda i: (i, 0)
    fix = lambda i: (0, 0)
    return pl.pallas_call(
        _ln_route_kernel, grid=(m // tm,),
        in_specs=[pl.BlockSpec((tm, d), row), pl.BlockSpec((1, d), fix), pl.BlockSpec((1, d), fix),
                  pl.BlockSpec((d, 128), fix), pl.BlockSpec((1, 128), fix)],
        out_specs=[pl.BlockSpec((tm, d), row), pl.BlockSpec((tm, d), row), pl.BlockSpec((tm, 128), row)],
        out_shape=[jax.ShapeDtypeStruct((m, d), F32), jax.ShapeDtypeStruct((m, d), BF16),
                   jax.ShapeDtypeStruct((m, 128), F32)],
        compiler_params=_params(("parallel",)), name="ln_route")(h, w, b, rw, rb)


def _combine_ln_kernel(x_ref, y_ref, rt_ref, w_ref, b_ref, o32_ref, o16_ref):
    d = x_ref.shape[1]
    rt = rt_ref[...]
    ffn = rt[:, 2:3] * y_ref[:, 0:d] + rt[:, 3:4] * y_ref[:, d:2 * d]
    y = _layer_norm(ALPHA * x_ref[...] + ffn, w_ref[...], b_ref[...])
    o32_ref[...] = y
    o16_ref[...] = y.astype(BF16)


def _combine_ln(x, ypair, route, w, b, tm=256):
    m, d = x.shape
    tm = min(tm, m)
    row = lambda i: (i, 0)
    fix = lambda i: (0, 0)
    return pl.pallas_call(
        _combine_ln_kernel, grid=(m // tm,),
        in_specs=[pl.BlockSpec((tm, d), row), pl.BlockSpec((tm, 2 * d), row), pl.BlockSpec((tm, 128), row),
                  pl.BlockSpec((1, d), fix), pl.BlockSpec((1, d), fix)],
        out_specs=[pl.BlockSpec((tm, d), row), pl.BlockSpec((tm, d), row)],
        out_shape=[jax.ShapeDtypeStruct((m, d), F32), jax.ShapeDtypeStruct((m, d), BF16)],
        compiler_params=_params(("parallel",)), name="combine_ln")(x, ypair, route, w, b)


def _moe_ffn_kernel(be_ref, bv_ref, x_ref, w1_ref, w3_ref, w2_ref, o_ref):
    i = pl.program_id(0)

    @pl.when(bv_ref[i] != 0)
    def _():
        x = x_ref[...]
        h = _silu(_dot(x, w1_ref[0])) * _dot(x, w3_ref[0])
        o_ref[...] = _dot(h.astype(BF16), w2_ref[0])

    @pl.when(bv_ref[i] == 0)
    def _():
        o_ref[...] = jnp.zeros_like(o_ref)


def _moe_ffn(xbuf, block_expert, block_valid, w1, w3, w2):
    rows, d = xbuf.shape
    nb = rows // MOE_BM
    de = w1.shape[2]
    grid_spec = pltpu.PrefetchScalarGridSpec(
        num_scalar_prefetch=2, grid=(nb,),
        in_specs=[pl.BlockSpec((MOE_BM, d), lambda i, be, bv: (i, 0)),
                  pl.BlockSpec((1, d, de), lambda i, be, bv: (be[i], 0, 0)),
                  pl.BlockSpec((1, d, de), lambda i, be, bv: (be[i], 0, 0)),
                  pl.BlockSpec((1, de, d), lambda i, be, bv: (be[i], 0, 0))],
        out_specs=pl.BlockSpec((MOE_BM, d), lambda i, be, bv: (i, 0)))
    return pl.pallas_call(
        _moe_ffn_kernel, grid_spec=grid_spec,
        out_shape=jax.ShapeDtypeStruct((rows, d), F32),
        compiler_params=_params(("arbitrary",)), name="moe_ffn")(block_expert, block_valid, xbuf, w1, w3, w2)


def _moe(x32, x16, route, w1, w3, w2, ln_w, ln_b):
    t, d = x32.shape
    n_assign = t * TOP_K
    nb = -(-n_assign // MOE_BM) + N_EXPERTS
    rows = nb * MOE_BM
    expert = route[:, 0:TOP_K].astype(jnp.int32).reshape(-1)
    onehot = (expert[:, None] == jnp.arange(N_EXPERTS, dtype=jnp.int32)[None, :]).astype(jnp.int32)
    csum = jnp.cumsum(onehot, axis=0)
    rank = jnp.sum(csum * onehot, axis=1) - 1
    counts = csum[-1]
    padded = (counts + MOE_BM - 1) // MOE_BM * MOE_BM
    pad_end = jnp.cumsum(padded)
    pos = (pad_end - padded)[expert] + rank
    tok_of_row = jnp.zeros((rows,), jnp.int32).at[pos].set(jnp.arange(n_assign, dtype=jnp.int32) // TOP_K)
    block_start = jnp.arange(nb, dtype=jnp.int32) * MOE_BM
    block_expert = jnp.minimum(jnp.searchsorted(pad_end, block_start, side='right'),
                               N_EXPERTS - 1).astype(jnp.int32)
    block_valid = (block_start < pad_end[-1]).astype(jnp.int32)
    xbuf = jnp.take(x16, tok_of_row, axis=0)
    ybuf = _moe_ffn(xbuf, block_expert, block_valid, w1, w3, w2)
    ypair = jnp.take(ybuf, pos, axis=0).reshape(t, TOP_K * d)
    return _combine_ln(x32, ypair, route, ln_w, ln_b)


def _mamba_kernel(xbc_ref, z_ref, sm_ref, cw_ref, cb_ref, dtb_ref, a_ref, d_ref, nw_ref, o_ref,
                  ext_ref, st_ref):
    lc = M_CHUNK
    hg = M_HEADS // M_GROUPS
    gw = hg * M_HEADDIM

    @pl.when(pl.program_id(1) == 0)
    def _():
        ext_ref[0:8, :] = jnp.zeros((8, M_XBC), F32)
        st_ref[...] = jnp.zeros_like(st_ref)

    ext_ref[8:8 + lc, :] = xbc_ref[...]
    acc = cb_ref[...] + cw_ref[0:1, :] * ext_ref[5:5 + lc, :]
    for k in range(1, M_CONV):
        acc = acc + cw_ref[k:k + 1, :] * ext_ref[5 + k:5 + k + lc, :]
    ext_ref[0:8, :] = ext_ref[lc:lc + 8, :]
    xc = _silu(acc)
    xs = xc[:, 0:BW]
    bm = xc[:, BW:BW + M_GROUPS * M_STATE].astype(BF16)
    cm = xc[:, BW + M_GROUPS * M_STATE:].astype(BF16)

    dtv = _softplus(sm_ref[:, 0:128] + dtb_ref[...])
    a = dtv * a_ref[...]
    row = _iota2((lc, lc), 0)
    col = _iota2((lc, lc), 1)
    causal = row >= col
    a_cs = _dot(causal.astype(F32), a, HI)
    a_cs_t = a_cs.T
    expand = _head_expand(128, BW, M_HEADDIM)
    ac = _dot(a_cs, expand, HI)
    dt_c = _dot(dtv, expand, HI)
    ac_last = ac[lc - 1:lc, :]
    xdt = xs * dt_c
    xdt_b = xdt.astype(BF16)
    xdte = (xdt * jnp.exp(ac_last - ac)).astype(BF16)

    y_parts = []
    for g in range(M_GROUPS):
        bm_g = bm[:, g * M_STATE:(g + 1) * M_STATE]
        cm_g = cm[:, g * M_STATE:(g + 1) * M_STATE]
        cb = _dot_nt(cm_g, bm_g)
        s_g = st_ref[g]
        y_off = _dot(cm_g, s_g.astype(BF16))
        for hh in range(hg):
            h = g * hg + hh
            seg = jnp.where(causal, jnp.exp(a_cs[:, h:h + 1] - a_cs_t[h:h + 1, :]), 0.0)
            y_parts.append(_dot((cb * seg).astype(BF16), xdt_b[:, h * M_HEADDIM:(h + 1) * M_HEADDIM]))
        st_ref[g] = (jnp.exp(ac_last[:, g * gw:(g + 1) * gw]) * s_g
                     + _dot_tn(bm_g, xdte[:, g * gw:(g + 1) * gw]))
        y_parts.append(y_off)
    y_diag = jnp.concatenate(y_parts[0:hg] + y_parts[hg + 1:2 * hg + 1], axis=1)
    y_off = jnp.concatenate([y_parts[hg], y_parts[2 * hg + 1]], axis=1)
    y = y_diag + y_off * jnp.exp(ac) + xs * d_ref[...]
    y = y * _silu(z_ref[...])
    outs = []
    for g in range(M_GROUPS):
        yg = y[:, g * gw:(g + 1) * gw]
        ms = jnp.mean(yg * yg, axis=-1, keepdims=True)
        outs.append(yg * lax.rsqrt(ms + 1e-6) * nw_ref[:, g * gw:(g + 1) * gw])
    o_ref[...] = jnp.concatenate(outs, axis=1).astype(o_ref.dtype)


def _mamba(proj, bsz, seqlen, conv_w, conv_b, dt_bias, a_log, d_skip, norm_w):
    lc = M_CHUNK
    nc = seqlen // lc
    pad = lambda v: jnp.zeros((1, 128), F32).at[0, :M_HEADS].set(v)
    vecs = [conv_w, conv_b.reshape(1, -1), pad(dt_bias), pad(-jnp.exp(a_log)),
            jnp.repeat(d_skip, M_HEADDIM).reshape(1, -1), norm_w.reshape(1, -1)]
    seg = lambda blk: (lambda b, c: (b * nc + c, blk))
    fix = lambda b, c: (0, 0)
    return pl.pallas_call(
        _mamba_kernel, grid=(bsz, nc),
        in_specs=[pl.BlockSpec((lc, M_XBC), seg(C_XBC // M_XBC)),
                  pl.BlockSpec((lc, BW), seg(C_Z // BW)),
                  pl.BlockSpec((lc, 512), seg(C_SM // 512))] + [pl.BlockSpec(v.shape, fix) for v in vecs],
        out_specs=pl.BlockSpec((lc, BW), seg(0)),
        out_shape=jax.ShapeDtypeStruct((bsz * seqlen, BW), BF16),
        scratch_shapes=[pltpu.VMEM((lc + 8, M_XBC), F32),
                        pltpu.VMEM((M_GROUPS, M_STATE, BW // M_GROUPS), F32)],
        compiler_params=_params(("parallel", "arbitrary")), name="mamba")(proj, proj, proj, *vecs)


def _s5_kernel(u_ref, bre_ref, bim_ref, cre_ref, cim_ref, pw_ref, d_ref, gw_ref, gb_ref, o_ref,
               hre_ref, him_ref, car_ref):
    tb = S5_TB
    nw = S5_GB * S5_STATE

    @pl.when(pl.program_id(1) == 0)
    def _():
        car_ref[...] = jnp.zeros_like(car_ref)

    u = u_ref[...]
    ub = u.astype(BF16)
    rmod = _iota2((tb, nw), 0) & 7
    ys = []
    for j in range(S5_GROUPS // S5_GB):
        uj = ub[:, j * 128:(j + 1) * 128]
        x_re = _dot(uj, bre_ref[j])
        x_im = _dot(uj, bim_ref[j])
        lanes = slice(j * nw, (j + 1) * nw)
        for s in (1, 2, 4):
            p_re = pw_ref[s - 1:s, lanes]
            p_im = pw_ref[8 + s - 1:8 + s, lanes]
            s_re = jnp.where(rmod >= s, pltpu.roll(x_re, s, axis=0), 0.0)
            s_im = jnp.where(rmod >= s, pltpu.roll(x_im, s, axis=0), 0.0)
            x_re, x_im = x_re + p_re * s_re - p_im * s_im, x_im + p_re * s_im + p_im * s_re
        q_re = pw_ref[0:8, lanes]
        q_im = pw_ref[8:16, lanes]
        c_re = car_ref[0:1, lanes]
        c_im = car_ref[1:2, lanes]
        for i in range(tb // 8):
            t_re = x_re[8 * i:8 * i + 8] + q_re * c_re - q_im * c_im
            t_im = x_im[8 * i:8 * i + 8] + q_re * c_im + q_im * c_re
            hre_ref[8 * i:8 * i + 8, :] = t_re
            him_ref[8 * i:8 * i + 8, :] = t_im
            c_re = t_re[7:8]
            c_im = t_im[7:8]
        car_ref[0:1, lanes] = c_re
        car_ref[1:2, lanes] = c_im
        ys.append(_dot(hre_ref[...].astype(BF16), cre_ref[j]) - _dot(him_ref[...].astype(BF16), cim_ref[j]))
    y = jnp.concatenate(ys, axis=1) + d_ref[...] * u
    y = jax.nn.gelu(y)
    o_ref[...] = (y * _sigmoid(_dot(y.astype(BF16), gw_ref[...]) + gb_ref[...])).astype(o_ref.dtype)


def _s5(proj, bsz, seqlen, lam_re, lam_im, b_re, b_im, c_re, c_im, d_skip, log_dt, glu_w, glu_b):
    tb = S5_TB
    nt = seqlen // tb
    nj = S5_GROUPS // S5_GB
    dt = jnp.exp(log_dt)[:, None]
    mag = jnp.exp(lam_re * dt)
    ab_re, ab_im = mag * jnp.cos(lam_im * dt), mag * jnp.sin(lam_im * dt)
    den = lam_re * lam_re + lam_im * lam_im
    q_re = ((ab_re - 1.0) * lam_re + ab_im * lam_im) / den
    q_im = (ab_im * lam_re - (ab_re - 1.0) * lam_im) / den
    bb_re = q_re[..., None] * b_re - q_im[..., None] * b_im
    bb_im = q_re[..., None] * b_im + q_im[..., None] * b_re
    eye = jnp.eye(S5_GB, dtype=F32)
    blk_b = lambda t: jnp.einsum('jgpc,gh->jgchp', t.reshape(nj, S5_GB, S5_STATE, S5_CH), eye).reshape(
        nj, S5_GB * S5_CH, S5_GB * S5_STATE).astype(BF16)
    blk_c = lambda t: jnp.einsum('jgcp,gh->jgphc', t.reshape(nj, S5_GB, S5_CH, S5_STATE), eye).reshape(
        nj, S5_GB * S5_STATE, S5_GB * S5_CH).astype(BF16)
    kk = jnp.arange(1, 9, dtype=F32)[:, None, None]
    pmag = jnp.exp(kk * (lam_re * dt)[None])
    pw = jnp.concatenate([pmag * jnp.cos(kk * (lam_im * dt)[None]), pmag * jnp.sin(kk * (lam_im * dt)[None])],
                         axis=0).reshape(16, S5_GROUPS * S5_STATE)
    args = [blk_b(bb_re), blk_b(bb_im), blk_c(c_re), blk_c(c_im), pw, d_skip.reshape(1, -1),
            glu_w.astype(BF16), glu_b.reshape(1, -1)]
    seg = lambda blk: (lambda b, c: (b * nt + c, blk))
    fixn = lambda nd: (lambda b, c: (0,) * nd)
    return pl.pallas_call(
        _s5_kernel, grid=(bsz, nt),
        in_specs=[pl.BlockSpec((tb, BW), seg(C_S5 // BW))] + [pl.BlockSpec(v.shape, fixn(v.ndim)) for v in args],
        out_specs=pl.BlockSpec((tb, BW), seg(0)),
        out_shape=jax.ShapeDtypeStruct((bsz * seqlen, BW), BF16),
        scratch_shapes=[pltpu.VMEM((tb, S5_GB * S5_STATE), F32), pltpu.VMEM((tb, S5_GB * S5_STATE), F32),
                        pltpu.VMEM((8, S5_GROUPS * S5_STATE), F32)],
        compiler_params=_params(("parallel", "arbitrary")), name="s5")(proj, *args)


def _unit_lower_inverse(l_ab, n):
    row = _iota2((n, n), 0)
    col = _iota2((n, n), 1)
    x = (row == col).astype(F32)
    s = 1
    while s < n:
        mask = (row // (2 * s) == col // (2 * s)) & ((row // s) % 2 == 1) & ((col // s) % 2 == 0)
        lm = jnp.where(mask, l_ab, 0.0)
        x = x + _dot(x, _dot(lm, x, HI), HI)
        s *= 2
    return x


def _rwkv_kernel(*refs, has_vmix):
    if has_vmix:
        (r_ref, k_ref, v_ref, sm_ref, vf_ref, mu_ref, msm_ref, vec_ref, w2_ref, a2_ref, g2_ref, v1_ref, v2_ref,
         o_ref, car_ref, csm_ref, st_ref, y_ref) = refs
    else:
        (r_ref, k_ref, v_ref, sm_ref, mu_ref, msm_ref, vec_ref, w2_ref, a2_ref, g2_ref,
         o_ref, vo_ref, car_ref, csm_ref, st_ref, y_ref) = refs
    c = R_CHUNK
    n = R_HEADDIM

    @pl.when(pl.program_id(1) == 0)
    def _():
        car_ref[...] = jnp.zeros_like(car_ref)
        csm_ref[...] = jnp.zeros_like(csm_ref)
        st_ref[...] = jnp.zeros_like(st_ref)

    def shift_mix(x, carry, mu):
        prev = jnp.where(_iota2(x.shape, 0) == 0, carry, pltpu.roll(x, 1, axis=0))
        return x + (prev - x) * mu

    r_raw, k_raw, v_raw, sm_raw = r_ref[...], k_ref[...], v_ref[...], sm_ref[...]
    r = shift_mix(r_raw, car_ref[0:1, :], mu_ref[0:1, :])
    k = shift_mix(k_raw, car_ref[1:2, :], mu_ref[1:2, :])
    v = shift_mix(v_raw, car_ref[2:3, :], mu_ref[2:3, :])
    sm = shift_mix(sm_raw, csm_ref[0:1, :], msm_ref[...])
    car_ref[0:1, :] = r_raw[c - 1:c, :]
    car_ref[1:2, :] = k_raw[c - 1:c, :]
    car_ref[2:3, :] = v_raw[c - 1:c, :]
    csm_ref[0:1, :] = sm_raw[c - 1:c, :]

    w0, a0, k_k, k_a, r_k, ln_w, ln_b = (vec_ref[i:i + 1, :] for i in range(7))
    lo = sm[:, SM_WLO:SM_WLO + 128]
    log_w = -_softplus(-(w0 + _dot(jnp.tanh(lo).astype(BF16), w2_ref[...]))) - 0.5
    lw = -jnp.exp(log_w)
    asig = _sigmoid(a0 + _dot(lo.astype(BF16), a2_ref[...]))
    gate = _dot(_sigmoid(sm[:, SM_GLO:SM_GLO + 256]).astype(BF16), g2_ref[...])
    if has_vmix:
        v0 = vec_ref[7:8, :]
        mix = _dot(_dot(v.astype(BF16), v1_ref[...]).astype(BF16), v2_ref[...])
        v = v + (vf_ref[...] - v) * _sigmoid(v0 + mix)
    else:
        vo_ref[...] = v

    e_red = _head_reduce(BW, 128, n)
    e_exp = _head_expand(128, BW, n)
    head_sum = lambda t: _dot(_dot(t, e_red, HI), e_exp, HI)
    kk = k * k_k
    kk = kk / jnp.maximum(jnp.sqrt(head_sum(kk * kk)), 1e-12)
    k = k * (1.0 + (asig - 1.0) * k_a)
    bonus = head_sum(r * k * r_k) * v

    row = _iota2((c, c), 0)
    col = _iota2((c, c), 1)
    lower = row >= col
    strict = row > col
    cs = _dot(lower.astype(F32), lw, HI)
    cs_last = cs[c - 1:c, :]
    w_in = jnp.exp(cs)
    w_inv = jnp.exp(-cs)
    w_end = jnp.exp(cs_last - cs)
    a_t = (-kk * jnp.exp(cs - lw)).astype(BF16)
    r_t = (r * w_in).astype(BF16)
    b_vec = kk * asig
    b_h = (b_vec * w_inv).astype(BF16)
    k_h = (k * w_inv).astype(BF16)
    b_e = (b_vec * w_end).astype(BF16)
    k_e = (k * w_end).astype(BF16)
    w_c = jnp.exp(cs_last)
    vb = v.astype(BF16)

    for h in range(R_HEADS):
        hs = slice(h * n, (h + 1) * n)
        ar = jnp.concatenate([a_t[:, hs], r_t[:, hs]], axis=0)
        bk = jnp.concatenate([b_h[:, hs], k_h[:, hs]], axis=0)
        g = _dot_nt(ar, bk)
        l_ab = jnp.where(strict, g[0:c, 0:c], 0.0)
        l_ak = jnp.where(strict, g[0:c, c:2 * c], 0.0)
        m_rb = jnp.where(lower, g[c:2 * c, 0:c], 0.0)
        m_rk = jnp.where(lower, g[c:2 * c, c:2 * c], 0.0)
        s0 = st_ref[h]
        p0 = _dot_nt(ar, s0.astype(BF16))
        v_h = vb[:, hs]
        rhs = p0[0:c] + _dot(l_ak.astype(BF16), v_h)
        sa = _dot(_unit_lower_inverse(l_ab, c), rhs, HI)
        sab = sa.astype(BF16)
        y_ref[:, hs] = p0[c:2 * c] + _dot(m_rb.astype(BF16), sab) + _dot(m_rk.astype(BF16), v_h)
        st_ref[h] = s0 * w_c[:, hs] + _dot_tn(sab, b_e[:, hs]) + _dot_tn(v_h, k_e[:, hs])

    y = y_ref[...]
    mean = head_sum(y) * (1.0 / n)
    yc = y - mean
    var = head_sum(yc * yc) * (1.0 / n)
    y = yc * lax.rsqrt(var + R_GN_EPS) * ln_w + ln_b
    o_ref[...] = ((y + bonus) * gate).astype(o_ref.dtype)


def _rwkv(proj, bsz, seqlen, v_first, mu, w0, w2, a0, a2, g2, k_k, k_a, r_k, ln_w, ln_b, v_mix):
    c = R_CHUNK
    nc = seqlen // c
    has_vmix = v_mix is not None
    mu_r, mu_k, mu_v, mu_w, mu_a, mu_g = jnp.split(mu, np.cumsum([BW, BW, BW, R_W_RANK, R_A_RANK]).tolist())
    mu_rkv = jnp.zeros((8, BW), F32).at[0].set(mu_r).at[1].set(mu_k).at[2].set(mu_v)
    mu_sm = jnp.zeros((1, 512), F32)
    mu_sm = mu_sm.at[0, SM_WLO:SM_WLO + R_W_RANK].set(mu_w).at[0, SM_ALO:SM_ALO + R_A_RANK].set(mu_a)
    mu_sm = mu_sm.at[0, SM_GLO:SM_GLO + R_G_RANK].set(mu_g)
    vec_rows = [w0, a0, k_k, k_a, r_k, ln_w, ln_b] + ([v_mix[0]] if has_vmix else [jnp.zeros((BW,), F32)])
    vecs = jnp.stack(vec_rows, axis=0)
    w2p = jnp.zeros((128, BW), F32).at[0:R_W_RANK].set(w2).astype(BF16)
    a2p = jnp.zeros((128, BW), F32).at[R_W_RANK:R_W_RANK + R_A_RANK].set(a2).astype(BF16)
    g2p = jnp.zeros((256, BW), F32).at[0:R_G_RANK].set(g2).astype(BF16)
    seg = lambda blk: (lambda b, i: (b * nc + i, blk))
    fix = lambda b, i: (0, 0)
    ins = [proj, proj, proj, proj]
    in_specs = [pl.BlockSpec((c, BW), seg(C_RR // BW)), pl.BlockSpec((c, BW), seg(C_RK // BW)),
                pl.BlockSpec((c, BW), seg(C_RV // BW)), pl.BlockSpec((c, 512), seg(C_SM // 512))]
    if has_vmix:
        ins.append(v_first)
        in_specs.append(pl.BlockSpec((c, BW), seg(0)))
    consts = [mu_rkv, mu_sm, vecs, w2p, a2p, g2p]
    if has_vmix:
        consts += [jnp.zeros((BW, 128), F32).at[:, 0:R_V_RANK].set(v_mix[1]).astype(BF16),
                   jnp.zeros((128, BW), F32).at[0:R_V_RANK].set(v_mix[2]).astype(BF16)]
    ins += consts
    in_specs += [pl.BlockSpec(v.shape, fix) for v in consts]
    out_spec = pl.BlockSpec((c, BW), seg(0))
    y_shape = jax.ShapeDtypeStruct((bsz * seqlen, BW), BF16)
    if has_vmix:
        out_specs, out_shape = out_spec, y_shape
    else:
        out_specs, out_shape = [out_spec, out_spec], [y_shape, jax.ShapeDtypeStruct((bsz * seqlen, BW), F32)]
    res = pl.pallas_call(
        functools.partial(_rwkv_kernel, has_vmix=has_vmix), grid=(bsz, nc),
        in_specs=in_specs, out_specs=out_specs, out_shape=out_shape,
        scratch_shapes=[pltpu.VMEM((8, BW), F32), pltpu.VMEM((8, 512), F32),
                        pltpu.VMEM((R_HEADS, R_HEADDIM, R_HEADDIM), F32), pltpu.VMEM((c, BW), F32)],
        compiler_params=_params(("parallel", "arbitrary")), name="rwkv")(*ins)
    if has_vmix:
        return res, v_first
    return res[0], res[1]


def _gla_kernel(q_ref, k_ref, v_ref, sm_ref, r_ref, a2_ref, ab_ref, nw_ref, o_ref, st_ref):
    c = G_CHUNK
    dk = G_DK // G_HEADS
    dv = G_DV // G_HEADS

    @pl.when(pl.program_id(1) == 0)
    def _():
        st_ref[...] = jnp.zeros_like(st_ref)

    la = _dot(sm_ref[:, 0:128].astype(BF16), a2_ref[...]) + ab_ref[...]
    log_a = -_softplus(-la) * (1.0 / G_TAU)
    lower = _iota2((c, c), 0) >= _iota2((c, c), 1)
    g_cs = _dot(lower.astype(F32), log_a, HI)
    g_last = g_cs[c - 1:c, :]
    k = k_ref[...]
    q_dec = (q_ref[...] * (dk ** -0.5) * jnp.exp(g_cs)).astype(BF16)
    k_inv = (k * jnp.exp(-g_cs)).astype(BF16)
    k_end = (k * jnp.exp(g_last - g_cs)).astype(BF16)
    s_decay = jnp.exp(g_last)
    vb = v_ref[...].astype(BF16)
    outs = []
    for h in range(G_HEADS):
        ks = slice(h * dk, (h + 1) * dk)
        vs = slice(h * dv, (h + 1) * dv)
        att = jnp.where(lower, _dot_nt(q_dec[:, ks], k_inv[:, ks]), 0.0)
        s0 = st_ref[h]
        o = _dot(att.astype(BF16), vb[:, vs]) + _dot_nt(q_dec[:, ks], s0.astype(BF16))
        st_ref[h] = s0 * s_decay[:, ks] + _dot_tn(vb[:, vs], k_end[:, ks])
        ms = jnp.mean(o * o, axis=-1, keepdims=True)
        outs.append(o * lax.rsqrt(ms + 1e-6) * nw_ref[...])
    o_ref[...] = (jnp.concatenate(outs, axis=1) * _silu(r_ref[...])).astype(o_ref.dtype)


def _gla(proj, bsz, seqlen, a2, a_b, norm_w):
    c = G_CHUNK
    nc = seqlen // c
    a2p = jnp.zeros((128, G_DK), F32).at[SM_GA:SM_GA + G_RANK].set(a2).astype(BF16)
    consts = [a2p, a_b.reshape(1, -1), norm_w.reshape(1, -1)]
    seg = lambda blk: (lambda b, i: (b * nc + i, blk))
    fix = lambda b, i: (0, 0)
    return pl.pallas_call(
        _gla_kernel, grid=(bsz, nc),
        in_specs=[pl.BlockSpec((c, G_DK), seg(C_GQ // G_DK)), pl.BlockSpec((c, G_DK), seg(C_GK // G_DK)),
                  pl.BlockSpec((c, G_DV), seg(C_GV // G_DV)), pl.BlockSpec((c, 512), seg(C_SM // 512)),
                  pl.BlockSpec((c, G_DV), seg(C_GR // G_DV))] + [pl.BlockSpec(v.shape, fix) for v in consts],
        out_specs=pl.BlockSpec((c, G_DV), seg(0)),
        out_shape=jax.ShapeDtypeStruct((bsz * seqlen, G_DV), BF16),
        scratch_shapes=[pltpu.VMEM((G_HEADS, G_DV // G_HEADS, G_DK // G_HEADS), F32)],
        compiler_params=_params(("parallel", "arbitrary")), name="gla")(proj, proj, proj, proj, proj, *consts)


def _reorder_cols(w):
    d = w.shape[0]
    sizes = [BW, M_XBC, M_HEADS, BW, BW, BW, BW, R_W_RANK, R_A_RANK, R_G_RANK, G_DK, G_DK, G_DV, G_RANK, G_DV]
    (z, xbc, dt, s5, rr, rk, rv, wlo, alo, glo, gq, gk, gv, ga, gr) = jnp.split(w, np.cumsum(sizes)[:-1].tolist(),
                                                                                 axis=1)
    zeros = lambda n: jnp.zeros((d, n), w.dtype)
    small = jnp.concatenate([dt, ga, zeros(128 - M_HEADS - G_RANK), wlo, alo, glo,
                             zeros(512 - SM_GLO - R_G_RANK)], axis=1)
    out = jnp.concatenate([xbc, gq, z, s5, rr, rk, rv, gv, gr, gk, small], axis=1)
    return out


def kernel(x, w_in, m_conv_w, m_conv_b, m_dt_bias, m_a_log, m_d, m_norm_w, s5_lam_re, s5_lam_im, s5_b_re,
           s5_b_im, s5_c_re, s5_c_im, s5_d, s5_log_dt, s5_glu_w, s5_glu_b, r_mu, r_w0, r_w2, r_a0, r_a2, r_g2,
           r_k_k, r_k_a, r_r_k, r_ln_w, r_ln_b, r_v0, r_v1, r_v2, g_a2, g_a_b, g_norm_w, w_branch, w_gate,
           b_gate, w_out, ln1_w, ln1_b, moe_rg_w, moe_rg_b, moe_re_w, moe_re_b, moe_w1, moe_w3, moe_w2,
           ln2_w, ln2_b):
    bsz, seqlen, d = x.shape
    x32 = x.reshape(bsz * seqlen, d)
    x16 = x32.astype(BF16)
    v_first = None
    for l in range(DEPTH):
        proj = _matmul(x16, _reorder_cols(w_in[l]).astype(BF16), F32)
        y_m = _mamba(proj, bsz, seqlen, m_conv_w[l], m_conv_b[l], m_dt_bias[l], m_a_log[l], m_d[l], m_norm_w[l])
        y_s = _s5(proj, bsz, seqlen, s5_lam_re[l], s5_lam_im[l], s5_b_re[l], s5_b_im[l], s5_c_re[l], s5_c_im[l],
                  s5_d[l], s5_log_dt[l], s5_glu_w[l], s5_glu_b[l])
        v_mix = None if l == 0 else (r_v0[l - 1], r_v1[l - 1], r_v2[l - 1])
        y_r, v_first = _rwkv(proj, bsz, seqlen, v_first, r_mu[l], r_w0[l], r_w2[l], r_a0[l], r_a2[l], r_g2[l],
                             r_k_k[l], r_k_a[l], r_r_k[l], r_ln_w[l], r_ln_b[l], v_mix)
        y_g = _gla(proj, bsz, seqlen, g_a2[l], g_a_b[l], g_norm_w[l])
        merged = _gate_merge(x16, (y_m, y_s, y_r, y_g), w_gate[l].astype(BF16), w_branch[l].astype(BF16),
                             b_gate[l].reshape(4, 1, d))
        h = _out_proj_residual(merged, w_out[l].astype(BF16), x32)
        rw = jnp.zeros((d, 128), F32).at[:, 0:N_GROUPS].set(moe_rg_w[l])
        rw = rw.at[:, N_GROUPS:N_GROUPS + N_EXPERTS].set(moe_re_w[l])
        rb = jnp.zeros((1, 128), F32).at[0, 0:N_GROUPS].set(moe_rg_b[l])
        rb = rb.at[0, N_GROUPS:N_GROUPS + N_EXPERTS].set(moe_re_b[l])
        x32, x16, route = _ln_route(h, ln1_w[l].reshape(1, d), ln1_b[l].reshape(1, d), rw, rb)
        x32, x16 = _moe(x32, x16, route, moe_w1[l].astype(BF16), moe_w3[l].astype(BF16), moe_w2[l].astype(BF16),
                        ln2_w[l].reshape(1, d), ln2_b[l].reshape(1, d))
    return x32.reshape(bsz, seqlen, d)
```
